```python
import jax, jax.numpy as jnp
from jax import lax
import numpy as np

D_MODEL = 2048
BATCH = 2
SEQ = 4096
DEPTH = 1

MIX_WIDTH = D_MODEL
POOL_WIDTH = MIX_WIDTH // 2
POOL_WINDOWS = (2, 4, 8, 16)
POOL_GROUP = POOL_WIDTH // len(POOL_WINDOWS)
FOX_WIDTH = MIX_WIDTH - POOL_WIDTH
FOX_HEADS = 8
FOX_HEAD_DIM = FOX_WIDTH // FOX_HEADS
Q_BLOCK = 128
IN_COLS = POOL_WIDTH + 3 * FOX_WIDTH + FOX_HEADS
N_MEM = 256
XATTN_HEADS = 4
XATTN_HEAD_DIM = D_MODEL // XATTN_HEADS
PEER_HEADS = 8
PEER_KEYS = 128
PEER_EXPERTS = PEER_KEYS * PEER_KEYS
PEER_QUERY_DIM = 256
PEER_HALF = PEER_QUERY_DIM // 2
PEER_TOPK = 16
PEER_TOKEN_CHUNK = 128
EPS = 1e-6

kernel_name = "hymba_pool_fox_peer_layer"


def rms_norm(x, g):
    xf = x.astype(jnp.float32)
    y = xf * lax.rsqrt(jnp.mean(xf * xf, axis=-1, keepdims=True) + EPS)
    return (y * g.astype(jnp.float32)).astype(x.dtype)


def causal_window_mean(x, w):
    S = x.shape[1]
    cs = jnp.pad(jnp.cumsum(x.astype(jnp.float32), axis=1), ((0, 0), (1, 0), (0, 0)))
    upper = cs[:, 1:]
    lower = jnp.pad(cs, ((0, 0), (w - 1, 0), (0, 0)))[:, :S]
    count = jnp.minimum(jnp.arange(S) + 1, w).astype(jnp.float32)[None, :, None]
    return ((upper - lower) / count).astype(x.dtype)


def pool_mixer(a, pool_w, pool_scale):
    B, S, _ = a.shape
    groups = a.reshape(B, S, len(POOL_WINDOWS), POOL_GROUP)
    diffs = jnp.stack([causal_window_mean(groups[:, :, i], w) - groups[:, :, i]
                       for i, w in enumerate(POOL_WINDOWS)], axis=2)
    y = jnp.einsum('bsgc,gcd->bsgd', diffs, pool_w).reshape(B, S, POOL_WIDTH)
    return y * pool_scale


def forgetting_attention(q, k, v, f_logit):
    B, S, H, Dh = q.shape
    c = jnp.cumsum(jax.nn.log_sigmoid(f_logit.astype(jnp.float32)), axis=1).transpose(0, 2, 1)
    qh = q.transpose(0, 2, 1, 3) * (Dh ** -0.5)
    kh = k.transpose(0, 2, 1, 3)
    vh = v.transpose(0, 2, 1, 3)
    outs = []
    for start in range(0, S, Q_BLOCK):
        end = start + Q_BLOCK
        s = jnp.einsum('bhqd,bhkd->bhqk', qh[:, :, start:end], kh[:, :, :end]).astype(jnp.float32)
        s = s + c[:, :, start:end, None] - c[:, :, None, :end]
        causal = jnp.arange(end)[None, :] <= jnp.arange(start, end)[:, None]
        s = jnp.where(causal, s, -jnp.inf)
        p = jax.nn.softmax(s, axis=-1).astype(vh.dtype)
        outs.append(jnp.einsum('bhqk,bhkd->bhqd', p, vh[:, :, :end]))
    o = jnp.concatenate(outs, axis=2)
    return o.transpose(0, 2, 1, 3).reshape(B, S, H * Dh)


def memory_cross_attention(h, m, wq, wk, wv, wo):
    B, S, D = h.shape
    M = m.shape[1]
    q = (h @ wq).reshape(B, S, XATTN_HEADS, XATTN_HEAD_DIM)
    k = (m @ wk).reshape(B, M, XATTN_HEADS, XATTN_HEAD_DIM)
    v = (m @ wv).reshape(B, M, XATTN_HEADS, XATTN_HEAD_DIM)
    s = jnp.einsum('bshd,bmhd->bhsm', q, k).astype(jnp.float32) * (XATTN_HEAD_DIM ** -0.5)
    p = jax.nn.softmax(s, axis=-1).astype(v.dtype)
    o = jnp.einsum('bhsm,bmhd->bshd', p, v).reshape(B, S, D)
    return o @ wo


def peer_ffn(h, w_query, sub_keys, expert_u, expert_v):
    B, S, D = h.shape
    q = (h @ w_query).reshape(B, S, PEER_HEADS, 2, PEER_HALF)
    scores = jnp.einsum('bshcd,hckd->bshck', q, sub_keys).astype(jnp.float32)
    s_top, i_top = lax.top_k(scores, PEER_TOPK)
    cand = s_top[..., 0, :, None] + s_top[..., 1, None, :]
    cand_idx = i_top[..., 0, :, None] * PEER_KEYS + i_top[..., 1, None, :]
    cand = cand.reshape(B, S, PEER_HEADS, PEER_TOPK * PEER_TOPK)
    cand_idx = cand_idx.reshape(B, S, PEER_HEADS, PEER_TOPK * PEER_TOPK)
    best, pos = lax.top_k(cand, PEER_TOPK)
    expert_idx = jnp.take_along_axis(cand_idx, pos, axis=-1)
    gate = jax.nn.softmax(best, axis=-1).astype(h.dtype)
    T = B * S
    n_chunks = T // PEER_TOKEN_CHUNK
    K = PEER_HEADS * PEER_TOPK
    xs = h.reshape(n_chunks, PEER_TOKEN_CHUNK, D)
    idx = expert_idx.reshape(n_chunks, PEER_TOKEN_CHUNK, K)
    gs = gate.reshape(n_chunks, PEER_TOKEN_CHUNK, K)

    def chunk_fn(args):
        xc, ic, gc = args
        u = jnp.take(expert_u, ic, axis=0)
        act = jax.nn.gelu(jnp.einsum('ckd,cd->ck', u, xc), approximate=False)
        vv = jnp.take(expert_v, ic, axis=0)
        return jnp.einsum('ck,ckd->cd', gc * act, vv)

    out = lax.map(chunk_fn, (xs, idx, gs))
    return out.reshape(B, S, D)


def setup_inputs(seed: int = 0) -> dict:
    key = jax.random.key(seed)
    ks = jax.random.split(key, 24)
    n = jax.random.normal
    D = D_MODEL
    f32 = jnp.float32
    return {
        "x": n(ks[0], (BATCH, SEQ, D), f32),
        "mem": n(ks[1], (BATCH, N_MEM, D), f32),
        "mix_norm": 1.0 + 0.05 * n(ks[2], (DEPTH, D), f32),
        "w_in": n(ks[3], (DEPTH, D, IN_COLS), f32) * D ** -0.5,
        "b_forget": 3.0 + 0.1 * n(ks[4], (DEPTH, FOX_HEADS), f32),
        "pool_w": n(ks[5], (DEPTH, len(POOL_WINDOWS), POOL_GROUP, POOL_GROUP), f32) * POOL_GROUP ** -0.5,
        "pool_scale": 1.0 + 0.1 * n(ks[6], (DEPTH, POOL_WIDTH), f32),
        "w_out": n(ks[7], (DEPTH, MIX_WIDTH, D), f32) * MIX_WIDTH ** -0.5,
        "xattn_norm": 1.0 + 0.05 * n(ks[8], (DEPTH, D), f32),
        "mem_norm": 1.0 + 0.05 * n(ks[9], (DEPTH, D), f32),
        "xattn_wq": n(ks[10], (DEPTH, D, D), f32) * D ** -0.5,
        "xattn_wk": n(ks[11], (DEPTH, D, D), f32) * D ** -0.5,
        "xattn_wv": n(ks[12], (DEPTH, D, D), f32) * D ** -0.5,
        "xattn_wo": n(ks[13], (DEPTH, D, D), f32) * D ** -0.5,
        "peer_norm": 1.0 + 0.05 * n(ks[14], (DEPTH, D), f32),
        "peer_wq": n(ks[15], (DEPTH, D, PEER_HEADS * PEER_QUERY_DIM), f32) * D ** -0.5,
        "peer_sub_keys": n(ks[16], (DEPTH, PEER_HEADS, 2, PEER_KEYS, PEER_HALF), f32) * PEER_HALF ** -0.5,
        "peer_u": n(ks[17], (DEPTH, PEER_EXPERTS, D), f32) * D ** -0.5,
        "peer_v": n(ks[18], (DEPTH, PEER_EXPERTS, D), f32) * 0.3,
        "final_norm": 1.0 + 0.05 * n(ks[19], (D,), f32),
    }


def reference(x, mem, mix_norm, w_in, b_forget, pool_w, pool_scale, w_out,
              xattn_norm, mem_norm, xattn_wq, xattn_wk, xattn_wv, xattn_wo,
              peer_norm, peer_wq, peer_sub_keys, peer_u, peer_v, final_norm):
    B, S, _ = x.shape
    h = x
    for l in range(DEPTH):
        a = rms_norm(h, mix_norm[l])
        proj = a @ w_in[l]
        o0 = POOL_WIDTH
        a_pool = proj[..., :o0]
        q = proj[..., o0:o0 + FOX_WIDTH].reshape(B, S, FOX_HEADS, FOX_HEAD_DIM)
        k = proj[..., o0 + FOX_WIDTH:o0 + 2 * FOX_WIDTH].reshape(B, S, FOX_HEADS, FOX_HEAD_DIM)
        v = proj[..., o0 + 2 * FOX_WIDTH:o0 + 3 * FOX_WIDTH].reshape(B, S, FOX_HEADS, FOX_HEAD_DIM)
        f_logit = proj[..., o0 + 3 * FOX_WIDTH:] + b_forget[l]
        y_pool = pool_mixer(a_pool, pool_w[l], pool_scale[l])
        y_fox = forgetting_attention(q, k, v, f_logit)
        h = h + jnp.concatenate([y_pool, y_fox], axis=-1) @ w_out[l]
        m = rms_norm(mem, mem_norm[l])
        h = h + memory_cross_attention(rms_norm(h, xattn_norm[l]), m,
                                       xattn_wq[l], xattn_wk[l], xattn_wv[l], xattn_wo[l])
        h = h + peer_ffn(rms_norm(h, peer_norm[l]), peer_wq[l], peer_sub_keys[l], peer_u[l], peer_v[l])
    return rms_norm(h, final_norm)
```

```python
import functools

import jax
import jax.numpy as jnp
from jax import lax
from jax.experimental import pallas as pl
from jax.experimental.pallas import tpu as pltpu

EPS = 1e-6
POOL_WINDOWS = (2, 4, 8, 16)
FOX_HEADS = 8
XATTN_HEADS = 4
PEER_HEADS = 8
PEER_KEYS = 128
PEER_TOPK = 16
LANES = 128
HALO = 16
NEG = -1e30
VMEM_LIMIT = 56 * 1024 * 1024

F32 = jnp.float32
BF16 = jnp.bfloat16


def _cparams(sem):
    return pltpu.CompilerParams(dimension_semantics=sem, vmem_limit_bytes=VMEM_LIMIT)


def _rms(xf, g):
    return xf * lax.rsqrt(jnp.mean(xf * xf, axis=-1, keepdims=True) + EPS) * g


def _const_spec(shape):
    nd = len(shape)
    return pl.BlockSpec(shape, lambda *_: (0,) * nd, pipeline_mode=pl.Buffered(1))


def _in_proj_kernel(x_ref, g_ref, w_ref, wf_ref, bf_ref, pool_ref, qkv_ref, f_ref, a_sc):
    j = pl.program_id(1)

    @pl.when(j == 0)
    def _():
        a = _rms(x_ref[...], g_ref[...]).astype(BF16)
        a_sc[...] = a
        f_ref[...] = jnp.dot(a, wf_ref[...], preferred_element_type=F32) + bf_ref[...]
        pool_ref[...] = jnp.dot(a, w_ref[...], preferred_element_type=F32)

    @pl.when(j > 0)
    def _():
        qkv_ref[...] = jnp.dot(a_sc[...], w_ref[...], preferred_element_type=F32).astype(BF16)


def _in_proj(x2, g, w_main, wf, bf, tt, cb):
    t, d = x2.shape
    ncb = w_main.shape[1] // cb
    return pl.pallas_call(
        _in_proj_kernel,
        grid=(t // tt, ncb),
        in_specs=[
            pl.BlockSpec((tt, d), lambda i, j: (i, 0)),
            pl.BlockSpec((1, d), lambda i, j: (0, 0)),
            pl.BlockSpec((d, cb), lambda i, j: (0, j)),
            pl.BlockSpec((d, LANES), lambda i, j: (0, 0)),
            pl.BlockSpec((1, LANES), lambda i, j: (0, 0)),
        ],
        out_specs=[
            pl.BlockSpec((tt, cb), lambda i, j: (i, 0)),
            pl.BlockSpec((tt, cb), lambda i, j: (i, jnp.maximum(j - 1, 0))),
            pl.BlockSpec((tt, LANES), lambda i, j: (i, 0)),
        ],
        out_shape=[
            jax.ShapeDtypeStruct((t, cb), F32),
            jax.ShapeDtypeStruct((t, (ncb - 1) * cb), BF16),
            jax.ShapeDtypeStruct((t, LANES), F32),
        ],
        scratch_shapes=[pltpu.VMEM((tt, d), BF16)],
        compiler_params=_cparams(("parallel", "arbitrary")),
        name="in_proj",
    )(x2, g, w_main, wf, bf)


def _decay_kernel(f_ref, c_ref, ct_ref, carry_sc, *, rb):
    r = pl.program_id(1)

    @pl.when(r == 0)
    def _():
        carry_sc[...] = jnp.zeros_like(carry_sc)

    f = f_ref[...]
    ls = jnp.minimum(f, 0.0) - jnp.log1p(jnp.exp(-jnp.abs(f)))
    row = lax.broadcasted_iota(jnp.int32, (rb, rb), 0)
    col = lax.broadcasted_iota(jnp.int32, (rb, rb), 1)
    tri = (col <= row).astype(F32)
    c = jnp.dot(tri, ls, preferred_element_type=F32, precision=lax.Precision.HIGHEST) + carry_sc[...]
    carry_sc[...] = c[rb - 1:rb, :]
    c_ref[...] = c
    ct_ref[...] = c.T


def _decay(f, b, s, rb):
    t = f.shape[0]
    nr = s // rb
    return pl.pallas_call(
        functools.partial(_decay_kernel, rb=rb),
        grid=(b, nr),
        in_specs=[pl.BlockSpec((rb, LANES), lambda bi, r: (bi * nr + r, 0))],
        out_specs=[
            pl.BlockSpec((rb, LANES), lambda bi, r: (bi * nr + r, 0)),
            pl.BlockSpec((None, LANES, rb), lambda bi, r: (bi, 0, r)),
        ],
        out_shape=[
            jax.ShapeDtypeStruct((t, LANES), F32),
            jax.ShapeDtypeStruct((b, LANES, s), F32),
        ],
        scratch_shapes=[pltpu.VMEM((1, LANES), F32)],
        compiler_params=_cparams(("parallel", "arbitrary")),
        name="decay",
    )(f)


def _fox_kernel(q_ref, k_ref, v_ref, c_ref, ct_ref, o_ref, m_sc, l_sc, acc_sc, cq_sc, *, scale, tq):
    h = pl.program_id(1)
    qi = pl.program_id(2)
    ki = pl.program_id(3)

    @pl.when(ki == 0)
    def _():
        m_sc[...] = jnp.full_like(m_sc, NEG)
        l_sc[...] = jnp.zeros_like(l_sc)
        acc_sc[...] = jnp.zeros_like(acc_sc)
        lane = lax.broadcasted_iota(jnp.int32, c_ref.shape, 1)
        cq_sc[...] = jnp.sum(jnp.where(lane == h, c_ref[...], 0.0), axis=1, keepdims=True)

    @pl.when(ki <= qi)
    def _():
        s = lax.dot_general(q_ref[...], k_ref[...], (((1,), (1,)), ((), ())),
                            preferred_element_type=F32) * scale
        s = s + cq_sc[...] - ct_ref[...]
        row = lax.broadcasted_iota(jnp.int32, (tq, tq), 0)
        col = lax.broadcasted_iota(jnp.int32, (tq, tq), 1)
        s = jnp.where((ki < qi) | (col <= row), s, NEG)
        m_prev = m_sc[...]
        m_new = jnp.maximum(m_prev, jnp.max(s, axis=1, keepdims=True))
        alpha = jnp.exp(m_prev - m_new)
        p = jnp.exp(s - m_new)
        l_sc[...] = alpha * l_sc[...] + jnp.sum(p, axis=1, keepdims=True)
        acc_sc[...] = alpha * acc_sc[...] + jnp.dot(p.astype(BF16), v_ref[...], preferred_element_type=F32)
        m_sc[...] = m_new

    @pl.when(ki == pl.num_programs(3) - 1)
    def _():
        o_ref[...] = (acc_sc[...] / l_sc[...]).astype(o_ref.dtype)


def _fox(qkv, c, ct4, b, s, dh, tq):
    t = qkv.shape[0]
    nh = FOX_HEADS
    nq = s // tq
    kern = functools.partial(_fox_kernel, scale=dh ** -0.5, tq=tq)
    return pl.pallas_call(
        kern,
        grid=(b, nh, nq, nq),
        in_specs=[
            pl.BlockSpec((tq, dh), lambda bi, h, qi, ki: (bi * nq + qi, h)),
            pl.BlockSpec((tq, dh), lambda bi, h, qi, ki: (bi * nq + jnp.minimum(ki, qi), nh + h)),
            pl.BlockSpec((tq, dh), lambda bi, h, qi, ki: (bi * nq + jnp.minimum(ki, qi), 2 * nh + h)),
            pl.BlockSpec((tq, LANES), lambda bi, h, qi, ki: (bi * nq + qi, 0)),
            pl.BlockSpec((None, None, 1, tq), lambda bi, h, qi, ki: (bi, h, 0, jnp.minimum(ki, qi))),
        ],
        out_specs=pl.BlockSpec((tq, dh), lambda bi, h, qi, ki: (bi * nq + qi, h)),
        out_shape=jax.ShapeDtypeStruct((t, nh * dh), BF16),
        scratch_shapes=[
            pltpu.VMEM((tq, 1), F32),
            pltpu.VMEM((tq, 1), F32),
            pltpu.VMEM((tq, dh), F32),
            pltpu.VMEM((tq, 1), F32),
        ],
        compiler_params=_cparams(("parallel", "parallel", "parallel", "arbitrary")),
        name="fox",
    )(qkv, qkv, qkv, c, ct4)


def _mix_out_kernel(pool_ref, halo_ref, fox_ref, x_ref, pw_ref, ps_ref, wo_ref, o_ref, ext_sc, *, tt, tiles_per_seq):
    i = pl.program_id(0)
    seq_tile = i % tiles_per_seq
    cur = pool_ref[...]
    ext_sc[pl.ds(HALO, tt), :] = cur
    ext_sc[pl.ds(0, HALO), :] = jnp.where(seq_tile == 0, 0.0, halo_ref[...])
    pos = seq_tile * tt + lax.broadcasted_iota(jnp.int32, (tt, 1), 0)
    pw = pool_ref.shape[1]
    grp = pw // len(POOL_WINDOWS)
    ys = []
    for g, w in enumerate(POOL_WINDOWS):
        cols = pl.ds(g * grp, grp)
        wsum = cur[:, g * grp:(g + 1) * grp]
        for k in range(1, w):
            wsum = wsum + ext_sc[pl.ds(HALO - k, tt), cols]
        count = jnp.minimum(pos + 1, w).astype(F32)
        diff = wsum / count - cur[:, g * grp:(g + 1) * grp]
        ys.append(jnp.dot(diff.astype(BF16), pw_ref[g], preferred_element_type=F32))
    y_pool = jnp.concatenate(ys, axis=1) * ps_ref[...]
    mixed = jnp.concatenate([y_pool.astype(BF16), fox_ref[...]], axis=1)
    o_ref[...] = x_ref[...] + jnp.dot(mixed, wo_ref[...], preferred_element_type=F32)


def _mix_out(pool, fox, x2, pool_w, pool_scale, w_out, s, tt):
    t, pw = pool.shape
    d = x2.shape[1]
    hb = tt // HALO
    kern = functools.partial(_mix_out_kernel, tt=tt, tiles_per_seq=s // tt)
    return pl.pallas_call(
        kern,
        grid=(t // tt,),
        in_specs=[
            pl.BlockSpec((tt, pw), lambda i: (i, 0)),
            pl.BlockSpec((HALO, pw), lambda i: (jnp.maximum(i * hb - 1, 0), 0)),
            pl.BlockSpec((tt, fox.shape[1]), lambda i: (i, 0)),
            pl.BlockSpec((tt, d), lambda i: (i, 0)),
            _const_spec(pool_w.shape),
            _const_spec(pool_scale.shape),
            _const_spec(w_out.shape),
        ],
        out_specs=pl.BlockSpec((tt, d), lambda i: (i, 0)),
        out_shape=jax.ShapeDtypeStruct((t, d), F32),
        scratch_shapes=[pltpu.VMEM((tt + HALO, pw), F32)],
        compiler_params=_cparams(("parallel",)),
        name="mix_out",
    )(pool, pool, fox, x2, pool_w, pool_scale, w_out)


def _mem_kv_kernel(m_ref, g_ref, wk_ref, wv_ref, k_ref, v_ref):
    m = _rms(m_ref[...], g_ref[...]).astype(BF16)
    k_ref[...] = jnp.dot(m, wk_ref[...].astype(BF16), preferred_element_type=F32).astype(BF16)
    v_ref[...] = jnp.dot(m, wv_ref[...].astype(BF16), preferred_element_type=F32).astype(BF16)


def _mem_kv(mem2, g, wk, wv, cb):
    n, d = mem2.shape
    return pl.pallas_call(
        _mem_kv_kernel,
        grid=(d // cb,),
        in_specs=[
            pl.BlockSpec((n, d), lambda j: (0, 0)),
            pl.BlockSpec((1, d), lambda j: (0, 0)),
            pl.BlockSpec((d, cb), lambda j: (0, j)),
            pl.BlockSpec((d, cb), lambda j: (0, j)),
        ],
        out_specs=[pl.BlockSpec((n, cb), lambda j: (0, j))] * 2,
        out_shape=[jax.ShapeDtypeStruct((n, d), BF16)] * 2,
        compiler_params=_cparams(("parallel",)),
        name="mem_kv",
    )(mem2, g, wk, wv)


def _xattn_kernel(h_ref, g_ref, wq_ref, k_ref, v_ref, wo_ref, o_ref, *, scale):
    h1 = h_ref[...]
    hn = _rms(h1, g_ref[...]).astype(BF16)
    q = jnp.dot(hn, wq_ref[...], preferred_element_type=F32).astype(BF16)
    d = q.shape[1]
    dh = d // XATTN_HEADS
    outs = []
    for hd in range(XATTN_HEADS):
        sl = slice(hd * dh, (hd + 1) * dh)
        s = lax.dot_general(q[:, sl], k_ref[:, sl], (((1,), (1,)), ((), ())),
                            preferred_element_type=F32) * scale
        p = jnp.exp(s - jnp.max(s, axis=1, keepdims=True))
        p = p / jnp.sum(p, axis=1, keepdims=True)
        outs.append(jnp.dot(p.astype(BF16), v_ref[:, sl], preferred_element_type=F32).astype(BF16))
    o = jnp.concatenate(outs, axis=1)
    o_ref[...] = h1 + jnp.dot(o, wo_ref[...], preferred_element_type=F32)


def _xattn(h1, g, wq, kmem, vmem, wo, s, n_mem, tt):
    t, d = h1.shape
    tps = s // tt
    kern = functools.partial(_xattn_kernel, scale=(d // XATTN_HEADS) ** -0.5)
    return pl.pallas_call(
        kern,
        grid=(t // tt,),
        in_specs=[
            pl.BlockSpec((tt, d), lambda i: (i, 0)),
            _const_spec(g.shape),
            _const_spec(wq.shape),
            pl.BlockSpec((n_mem, d), lambda i: (i // tps, 0)),
            pl.BlockSpec((n_mem, d), lambda i: (i // tps, 0)),
            _const_spec(wo.shape),
        ],
        out_specs=pl.BlockSpec((tt, d), lambda i: (i, 0)),
        out_shape=jax.ShapeDtypeStruct((t, d), F32),
        compiler_params=_cparams(("parallel",)),
        name="xattn",
    )(h1, g, wq, kmem, vmem, wo)


def _top16(s, lane_s, lane_o):
    vals = jnp.zeros(lane_o.shape, F32)
    idxs = jnp.zeros(lane_o.shape, jnp.int32)
    for r in range(PEER_TOPK):
        m = jnp.max(s, axis=1, keepdims=True)
        am = jnp.min(jnp.where(s == m, lane_s.astype(F32), 1e9), axis=1, keepdims=True).astype(jnp.int32)
        vals = jnp.where(lane_o == r, m, vals)
        idxs = jnp.where(lane_o == r, am, idxs)
        s = jnp.where(lane_s == am, -jnp.inf, s)
    return vals, idxs


def _route_kernel(h_ref, g_ref, wq_ref, keys_ref, xn_ref, i_ref, j_ref, gate_ref):
    tt = h_ref.shape[0]
    xn = _rms(h_ref[...], g_ref[...]).astype(BF16)
    xn_ref[...] = xn
    q = jnp.dot(xn, wq_ref[...], preferred_element_type=F32).astype(BF16)
    lane = lax.broadcasted_iota(jnp.int32, (tt, LANES), 1)
    lane2 = lax.broadcasted_iota(jnp.int32, (tt, 2 * LANES), 1)
    rep_lo = lane // PEER_TOPK
    rep_hi = rep_lo + LANES // PEER_TOPK
    til = lane % PEER_TOPK
    i_all = jnp.zeros((tt, LANES), jnp.int32)
    j_all = jnp.zeros((tt, LANES), jnp.int32)
    g_all = jnp.zeros((tt, LANES), F32)
    for hd in range(PEER_HEADS):
        tops = []
        for c in range(2):
            qc = q[:, (2 * hd + c) * LANES:(2 * hd + c + 1) * LANES]
            sc = lax.dot_general(qc, keys_ref[2 * hd + c], (((1,), (1,)), ((), ())),
                                 preferred_element_type=F32)
            tops.append(_top16(sc, lane, lane))
        (s0, i0), (s1, i1) = tops
        s1t = jnp.take_along_axis(s1, til, axis=1)
        cand = jnp.concatenate([jnp.take_along_axis(s0, rep_lo, axis=1) + s1t,
                                jnp.take_along_axis(s0, rep_hi, axis=1) + s1t], axis=1)
        best, pos = _top16(cand, lane2, lane)
        pos = jnp.where(lane < PEER_TOPK, pos, 0)
        ei = jnp.take_along_axis(i0, pos // PEER_TOPK, axis=1)
        ej = jnp.take_along_axis(i1, pos % PEER_TOPK, axis=1)
        valid = lane < PEER_TOPK
        e = jnp.where(valid, jnp.exp(best - jnp.max(jnp.where(valid, best, -jnp.inf), axis=1, keepdims=True)), 0.0)
        gate = e / jnp.sum(e, axis=1, keepdims=True)
        here = (lane >= hd * PEER_TOPK) & (lane < (hd + 1) * PEER_TOPK)
        shift = hd * PEER_TOPK
        if shift:
            ei = pltpu.roll(ei, shift, axis=1)
            ej = pltpu.roll(ej, shift, axis=1)
            gate = pltpu.roll(gate, shift, axis=1)
        i_all = jnp.where(here, ei, i_all)
        j_all = jnp.where(here, ej, j_all)
        g_all = jnp.where(here, gate, g_all)
    i_ref[...] = i_all
    j_ref[...] = j_all
    gate_ref[...] = g_all


def _route(h2, g, wq, keys, tt):
    t, d = h2.shape
    return pl.pallas_call(
        _route_kernel,
        grid=(t // tt,),
        in_specs=[
            pl.BlockSpec((tt, d), lambda i: (i, 0)),
            _const_spec(g.shape),
            _const_spec(wq.shape),
            _const_spec(keys.shape),
        ],
        out_specs=[
            pl.BlockSpec((tt, d), lambda i: (i, 0)),
            pl.BlockSpec((tt, LANES), lambda i: (i, 0)),
            pl.BlockSpec((tt, LANES), lambda i: (i, 0)),
            pl.BlockSpec((tt, LANES), lambda i: (i, 0)),
        ],
        out_shape=[
            jax.ShapeDtypeStruct((t, d), BF16),
            jax.ShapeDtypeStruct((t, LANES), jnp.int32),
            jax.ShapeDtypeStruct((t, LANES), jnp.int32),
            jax.ShapeDtypeStruct((t, LANES), F32),
        ],
        compiler_params=_cparams(("parallel",)),
        name="route",
    )(h2, g, wq, keys)


def _peer_act_kernel(xn_ref, u_ref, i_ref, j_ref, a_ref, *, eb):
    jb = pl.program_id(1)

    @pl.when(jb == 0)
    def _():
        a_ref[...] = jnp.zeros_like(a_ref)

    dense = lax.dot_general(xn_ref[...], u_ref[...], (((1,), (1,)), ((), ())),
                            preferred_element_type=F32)
    iv = i_ref[...]
    jv = j_ref[...]
    acc = a_ref[...]
    nsub = eb // PEER_KEYS
    for c in range(nsub):
        picked = jnp.take_along_axis(dense[:, c * PEER_KEYS:(c + 1) * PEER_KEYS], jv, axis=1)
        acc = jnp.where(iv == jb * nsub + c, picked, acc)
    a_ref[...] = acc


def _peer_act(xn, u, ei, ej, tt, eb):
    t, d = xn.shape
    ne = u.shape[0]
    return pl.pallas_call(
        functools.partial(_peer_act_kernel, eb=eb),
        grid=(t // tt, ne // eb),
        in_specs=[
            pl.BlockSpec((tt, d), lambda i, j: (i, 0)),
            pl.BlockSpec((eb, d), lambda i, j: (j, 0)),
            pl.BlockSpec((tt, LANES), lambda i, j: (i, 0)),
            pl.BlockSpec((tt, LANES), lambda i, j: (i, 0)),
        ],
        out_specs=pl.BlockSpec((tt, LANES), lambda i, j: (i, 0)),
        out_shape=jax.ShapeDtypeStruct((t, LANES), F32),
        compiler_params=_cparams(("parallel", "arbitrary")),
        name="peer_act",
    )(xn, u, ei, ej)


def _peer_out_kernel(h_ref, gate_ref, a_ref, i_ref, j_ref, v_ref, fn_ref, o_ref,
                     w_sc, wbuf_sc, acc_sc, *, tt, eb, stride):
    jb = pl.program_id(1)

    @pl.when(jb == 0)
    def _():
        a = a_ref[...]
        gelu = 0.5 * a * (1.0 + lax.erf(a * (2.0 ** -0.5)))
        w_sc[...] = gate_ref[...] * gelu
        acc_sc[...] = jnp.zeros_like(acc_sc)
        sub = lax.broadcasted_iota(jnp.int32, (PEER_KEYS, LANES), 0)

        def build(t, carry):
            irow = jnp.broadcast_to(i_ref[pl.ds(t, 1), :], (PEER_KEYS, LANES))
            jrow = jnp.broadcast_to(j_ref[pl.ds(t, 1), :], (PEER_KEYS, LANES))
            wrow = jnp.broadcast_to(w_sc[pl.ds(t, 1), :], (PEER_KEYS, LANES))
            first = jnp.where(sub == irow, wrow, 0.0).astype(BF16)
            second = jnp.where(sub == jrow, 1.0, 0.0).astype(BF16)
            wt = lax.dot_general(first, second, (((1,), (1,)), ((), ())),
                                 preferred_element_type=F32)
            wbuf_sc[pl.ds(t, PEER_KEYS, stride=stride), :] = wt
            return carry

        lax.fori_loop(0, tt, build, 0)

    nsub = eb // PEER_KEYS
    parts = []
    for c in range(nsub):
        start = pl.multiple_of((jb * nsub + c) * stride, 8)
        parts.append(wbuf_sc[pl.ds(start, tt), :].astype(BF16))
    lhs = jnp.concatenate(parts, axis=1)
    acc_sc[...] += jnp.dot(lhs, v_ref[...], preferred_element_type=F32)

    @pl.when(jb == pl.num_programs(1) - 1)
    def _():
        o_ref[...] = _rms(h_ref[...] + acc_sc[...], fn_ref[...])


def _peer_out(h2, gate, a, ei, ej, v, fnorm, tt, eb):
    t, d = h2.shape
    ne = v.shape[0]
    stride = tt + 8
    kern = functools.partial(_peer_out_kernel, tt=tt, eb=eb, stride=stride)
    tok = lambda i, j: (i, 0)
    return pl.pallas_call(
        kern,
        grid=(t // tt, ne // eb),
        in_specs=[
            pl.BlockSpec((tt, d), tok),
            pl.BlockSpec((tt, LANES), tok),
            pl.BlockSpec((tt, LANES), tok),
            pl.BlockSpec((tt, LANES), tok),
            pl.BlockSpec((tt, LANES), tok),
            pl.BlockSpec((eb, d), lambda i, j: (j, 0)),
            pl.BlockSpec((1, d), lambda i, j: (0, 0)),
        ],
        out_specs=pl.BlockSpec((tt, d), tok),
        out_shape=jax.ShapeDtypeStruct((t, d), F32),
        scratch_shapes=[
            pltpu.VMEM((tt, LANES), F32),
            pltpu.VMEM((PEER_KEYS * stride, LANES), F32),
            pltpu.VMEM((tt, d), F32),
        ],
        compiler_params=_cparams(("parallel", "arbitrary")),
        name="peer_out",
    )(h2, gate, a, ei, ej, v, fnorm)


def _layer(h, mem2, p, b, s, n_mem):
    d = h.shape[1]
    pw = p["pool_scale"].shape[0]
    fw = p["w_in"].shape[1] - pw - FOX_HEADS
    fox_w = fw // 3
    dh = fox_w // FOX_HEADS

    w_main = p["w_in"][:, :pw + fw].astype(BF16)
    wf = jnp.pad(p["w_in"][:, pw + fw:], ((0, 0), (0, LANES - FOX_HEADS))).astype(BF16)
    bf = jnp.pad(p["b_forget"], (0, LANES - FOX_HEADS)).reshape(1, LANES)
    pool, qkv, f = _in_proj(h, p["mix_norm"].reshape(1, d), w_main, wf, bf, tt=512, cb=pw)
    c, ct = _decay(f, b, s, rb=256)
    y_fox = _fox(qkv, c, ct.reshape(b, LANES, 1, s), b, s, dh, tq=512)
    h1 = _mix_out(pool, y_fox, h, p["pool_w"].astype(BF16), p["pool_scale"].reshape(1, pw),
                  p["w_out"].astype(BF16), s, tt=256)

    kmem, vmem = _mem_kv(mem2, p["mem_norm"].reshape(1, d), p["xattn_wk"], p["xattn_wv"], cb=512)
    h2 = _xattn(h1, p["xattn_norm"].reshape(1, d), p["xattn_wq"].astype(BF16), kmem, vmem,
                p["xattn_wo"].astype(BF16), s, n_mem, tt=256)

    keys = p["peer_sub_keys"].reshape(2 * PEER_HEADS, PEER_KEYS, -1).astype(BF16)
    xn, ei, ej, gate = _route(h2, p["peer_norm"].reshape(1, d), p["peer_wq"].astype(BF16), keys, tt=256)
    a = _peer_act(xn, p["peer_u"].astype(BF16), ei, ej, tt=1024, eb=512)
    return h2, gate, a, ei, ej


def kernel(x, mem, mix_norm, w_in, b_forget, pool_w, pool_scale, w_out, xattn_norm, mem_norm,
           xattn_wq, xattn_wk, xattn_wv, xattn_wo, peer_norm, peer_wq, peer_sub_keys, peer_u, peer_v,
           final_norm):
    b, s, d = x.shape
    n_mem = mem.shape[1]
    assert mix_norm.shape[0] == 1, "only depth 1 is supported"
    h = x.reshape(b * s, d)
    mem2 = mem.reshape(b * n_mem, d)
    p = dict(mix_norm=mix_norm[0], w_in=w_in[0], b_forget=b_forget[0], pool_w=pool_w[0],
             pool_scale=pool_scale[0], w_out=w_out[0], xattn_norm=xattn_norm[0], mem_norm=mem_norm[0],
             xattn_wq=xattn_wq[0], xattn_wk=xattn_wk[0], xattn_wv=xattn_wv[0], xattn_wo=xattn_wo[0],
             peer_norm=peer_norm[0], peer_wq=peer_wq[0], peer_sub_keys=peer_sub_keys[0],
             peer_u=peer_u[0])
    h2, gate, a, ei, ej = _layer(h, mem2, p, b, s, n_mem)
    out = _peer_out(h2, gate, a, ei, ej, peer_v[0].astype(BF16), final_norm.reshape(1, d), tt=256, eb=1024)
    return out.reshape(b, s, d)
```

```python
import functools
import math

import numpy as np
import jax
import jax.numpy as jnp
from jax import lax
from jax.experimental import pallas as pl
from jax.experimental.pallas import tpu as pltpu

EPS = 1e-6
POOL_WINDOWS = (2, 4, 8, 16)
FOX_HEADS = 8
XATTN_HEADS = 4
PEER_HEADS = 8
PEER_KEYS = 128
PEER_TOPK = 16
LANES = 128
SUBLANES = 8
HALO = 16
NEG = -1e30
LOG2E = math.log2(math.e)
VMEM_LIMIT = 56 * 1024 * 1024

F32 = jnp.float32
BF16 = jnp.bfloat16

CAND_COUNTS = tuple(PEER_TOPK // (a + 1) for a in range(PEER_TOPK))
CAND_OFFSETS = tuple(sum(CAND_COUNTS[:a]) for a in range(PEER_TOPK))
CAND_TOTAL = sum(CAND_COUNTS)
CAND_ROWS = -(-CAND_TOTAL // SUBLANES) * SUBLANES


def _cparams(sem):
    return pltpu.CompilerParams(dimension_semantics=sem, vmem_limit_bytes=VMEM_LIMIT)


def _rms(xf, g):
    return xf * lax.rsqrt(jnp.mean(xf * xf, axis=-1, keepdims=True) + EPS) * g


def _const_spec(shape):
    nd = len(shape)
    return pl.BlockSpec(shape, lambda *_: (0,) * nd, pipeline_mode=pl.Buffered(1))


def _in_proj_kernel(x_ref, g_ref, w_ref, wf_ref, bf_ref, pool_ref, qkv_ref, f_ref, a_sc, *, q_scale):
    j = pl.program_id(1)

    @pl.when(j == 0)
    def _():
        a = _rms(x_ref[...], g_ref[...]).astype(BF16)
        a_sc[...] = a
        f_ref[...] = jnp.dot(a, wf_ref[...], preferred_element_type=F32) + bf_ref[...]
        pool_ref[...] = jnp.dot(a, w_ref[...], preferred_element_type=F32)

    @pl.when(j > 0)
    def _():
        res = jnp.dot(a_sc[...], w_ref[...], preferred_element_type=F32)
        qkv_ref[...] = (res * jnp.where(j == 1, q_scale, 1.0)).astype(BF16)


def _in_proj(x2, g, w_main, wf, bf, tt, cb, q_scale):
    t, d = x2.shape
    ncb = w_main.shape[1] // cb
    return pl.pallas_call(
        functools.partial(_in_proj_kernel, q_scale=q_scale),
        grid=(t // tt, ncb),
        in_specs=[
            pl.BlockSpec((tt, d), lambda i, j: (i, 0)),
            pl.BlockSpec((1, d), lambda i, j: (0, 0)),
            pl.BlockSpec((d, cb), lambda i, j: (0, j)),
            pl.BlockSpec((d, LANES), lambda i, j: (0, 0)),
            pl.BlockSpec((1, LANES), lambda i, j: (0, 0)),
        ],
        out_specs=[
            pl.BlockSpec((tt, cb), lambda i, j: (i, 0)),
            pl.BlockSpec((tt, cb), lambda i, j: (i, jnp.maximum(j - 1, 0))),
            pl.BlockSpec((tt, LANES), lambda i, j: (i, 0)),
        ],
        out_shape=[
            jax.ShapeDtypeStruct((t, cb), F32),
            jax.ShapeDtypeStruct((t, (ncb - 1) * cb), BF16),
            jax.ShapeDtypeStruct((t, LANES), F32),
        ],
        scratch_shapes=[pltpu.VMEM((tt, d), BF16)],
        compiler_params=_cparams(("parallel", "arbitrary")),
        name="in_proj",
    )(x2, g, w_main, wf, bf)


C_PARTS = 3


def _decay_kernel(f_ref, eq_ref, ek_ref, oq_ref, ok_ref, qx_ref, kx_ref, carry_sc, *, rb):
    r = pl.program_id(1)

    @pl.when(r == 0)
    def _():
        carry_sc[...] = jnp.zeros_like(carry_sc)

    f = f_ref[...]
    ls = jnp.minimum(f, 0.0) - jnp.log1p(jnp.exp(-jnp.abs(f)))
    row = lax.broadcasted_iota(jnp.int32, (rb, rb), 0)
    col = lax.broadcasted_iota(jnp.int32, (rb, rb), 1)
    tri = (col <= row).astype(F32)
    c = jnp.dot(tri, ls, preferred_element_type=F32, precision=lax.Precision.HIGHEST) + carry_sc[...]
    carry_sc[...] = c[rb - 1:rb, :]
    rest = c * LOG2E
    parts = []
    for _ in range(C_PARTS):
        piece = rest.astype(BF16)
        parts.append(piece)
        rest = rest - piece.astype(F32)
    parts = jnp.concatenate(parts, axis=1)
    qx_ref[...] = (jnp.dot(parts, eq_ref[...], preferred_element_type=F32) + oq_ref[...]).astype(BF16)
    kx_ref[...] = (ok_ref[...] - jnp.dot(parts, ek_ref[...], preferred_element_type=F32)).astype(BF16)


def _decay_constants(dh):
    eq = np.zeros((C_PARTS * LANES, FOX_HEADS * dh), np.float32)
    ek = np.zeros_like(eq)
    oq = np.zeros((1, FOX_HEADS * dh), np.float32)
    ok = np.zeros_like(oq)
    for h in range(FOX_HEADS):
        for p in range(C_PARTS):
            eq[p * LANES + h, h * dh + p] = 1.0
            ek[p * LANES + h, h * dh + C_PARTS + p] = 1.0
            oq[0, h * dh + C_PARTS + p] = 1.0
            ok[0, h * dh + p] = 1.0
    return jnp.asarray(eq, BF16), jnp.asarray(ek, BF16), jnp.asarray(oq), jnp.asarray(ok)


def _decay(f, b, s, dh, rb):
    t = f.shape[0]
    nr = s // rb
    consts = _decay_constants(dh)
    w = FOX_HEADS * dh
    return pl.pallas_call(
        functools.partial(_decay_kernel, rb=rb),
        grid=(b, nr),
        in_specs=[pl.BlockSpec((rb, LANES), lambda bi, r: (bi * nr + r, 0))]
        + [_const_spec(c.shape) for c in consts],
        out_specs=[pl.BlockSpec((rb, w), lambda bi, r: (bi * nr + r, 0))] * 2,
        out_shape=[jax.ShapeDtypeStruct((t, w), BF16)] * 2,
        scratch_shapes=[pltpu.VMEM((1, LANES), F32)],
        compiler_params=_cparams(("parallel", "arbitrary")),
        name="decay",
    )(f, *consts)


def _fox_kernel(q_ref, qx_ref, k_ref, kx_ref, v_ref, o_ref, m_sc, acc_sc, *, tq, dh, hg):
    qi = pl.program_id(2)
    ki = pl.program_id(3)

    @pl.when(ki == 0)
    def _():
        m_sc[...] = jnp.full_like(m_sc, NEG)
        acc_sc[...] = jnp.zeros_like(acc_sc)

    def step(diagonal):
        if diagonal:
            row = lax.broadcasted_iota(jnp.int32, (tq, tq), 0)
            col = lax.broadcasted_iota(jnp.int32, (tq, tq), 1)
            keep = col <= row
        ones = jnp.ones((tq, dh), BF16)
        for hh in range(hg):
            sl = slice(hh * dh, (hh + 1) * dh)
            qa = jnp.concatenate([q_ref[:, sl], qx_ref[:, sl]], axis=1)
            ka = jnp.concatenate([k_ref[:, sl], kx_ref[:, sl]], axis=1)
            s = lax.dot_general(qa, ka, (((1,), (1,)), ((), ())), preferred_element_type=F32)
            if diagonal:
                s = jnp.where(keep, s, NEG)
            m_prev = m_sc[hh]
            m_cur = jnp.broadcast_to(jnp.max(s, axis=1, keepdims=True), m_prev.shape)
            m_new = jnp.maximum(m_prev, m_cur)
            alpha = jnp.exp2(m_prev - m_new)
            p = jnp.exp2(s - jnp.concatenate([m_new] * (tq // LANES), axis=1))
            va = jnp.concatenate([v_ref[:, sl], ones], axis=1)
            acc_sc[hh] = (jnp.concatenate([alpha] * (2 * dh // LANES), axis=1) * acc_sc[hh]
                          + jnp.dot(p.astype(BF16), va, preferred_element_type=F32))
            m_sc[hh] = m_new

    @pl.when(ki < qi)
    def _():
        step(False)

    @pl.when(ki == qi)
    def _():
        step(True)

    @pl.when(ki == pl.num_programs(3) - 1)
    def _():
        for hh in range(hg):
            acc = acc_sc[hh]
            o_ref[:, hh * dh:(hh + 1) * dh] = (acc[:, :dh] / acc[:, dh:]).astype(o_ref.dtype)


def _fox(qkv, qx, kx, b, s, dh, tq, hg):
    t = qkv.shape[0]
    ng = FOX_HEADS // hg
    nq = s // tq
    kern = functools.partial(_fox_kernel, tq=tq, dh=dh, hg=hg)
    qrow = lambda bi, g, qi, ki: (bi * nq + qi, g)
    krow = lambda off: (lambda bi, g, qi, ki: (bi * nq + jnp.minimum(ki, qi), off + g))
    blk = (tq, hg * dh)
    return pl.pallas_call(
        kern,
        grid=(b, ng, nq, nq),
        in_specs=[
            pl.BlockSpec(blk, qrow),
            pl.BlockSpec(blk, qrow),
            pl.BlockSpec(blk, krow(ng)),
            pl.BlockSpec(blk, krow(0)),
            pl.BlockSpec(blk, krow(2 * ng)),
        ],
        out_specs=pl.BlockSpec(blk, qrow),
        out_shape=jax.ShapeDtypeStruct((t, FOX_HEADS * dh), BF16),
        scratch_shapes=[
            pltpu.VMEM((hg, tq, LANES), F32),
            pltpu.VMEM((hg, tq, 2 * dh), F32),
        ],
        compiler_params=_cparams(("parallel", "parallel", "parallel", "arbitrary")),
        name="fox",
    )(qkv, qx, qkv, kx, qkv)


def _mix_out_kernel(pool_ref, halo_ref, fox_ref, x_ref, pw_ref, ps_ref, wo_ref, o_ref, ext_sc, *, tt, tiles_per_seq):
    i = pl.program_id(0)
    seq_tile = i % tiles_per_seq
    cur = pool_ref[...]
    ext_sc[pl.ds(HALO, tt), :] = cur
    ext_sc[pl.ds(0, HALO), :] = jnp.where(seq_tile == 0, 0.0, halo_ref[...])
    pos = seq_tile * tt + lax.broadcasted_iota(jnp.int32, (tt, 1), 0)
    pw = pool_ref.shape[1]
    grp = pw // len(POOL_WINDOWS)
    ys = []
    for g, w in enumerate(POOL_WINDOWS):
        cols = pl.ds(g * grp, grp)
        wsum = cur[:, g * grp:(g + 1) * grp]
        for k in range(1, w):
            wsum = wsum + ext_sc[pl.ds(HALO - k, tt), cols]
        count = jnp.minimum(pos + 1, w).astype(F32)
        diff = wsum / count - cur[:, g * grp:(g + 1) * grp]
        ys.append(jnp.dot(diff.astype(BF16), pw_ref[g], preferred_element_type=F32))
    y_pool = jnp.concatenate(ys, axis=1) * ps_ref[...]
    mixed = jnp.concatenate([y_pool.astype(BF16), fox_ref[...]], axis=1)
    o_ref[...] = x_ref[...] + jnp.dot(mixed, wo_ref[...], preferred_element_type=F32)


def _mix_out(pool, fox, x2, pool_w, pool_scale, w_out, s, tt):
    t, pw = pool.shape
    d = x2.shape[1]
    hb = tt // HALO
    kern = functools.partial(_mix_out_kernel, tt=tt, tiles_per_seq=s // tt)
    return pl.pallas_call(
        kern,
        grid=(t // tt,),
        in_specs=[
            pl.BlockSpec((tt, pw), lambda i: (i, 0)),
            pl.BlockSpec((HALO, pw), lambda i: (jnp.maximum(i * hb - 1, 0), 0)),
            pl.BlockSpec((tt, fox.shape[1]), lambda i: (i, 0)),
            pl.BlockSpec((tt, d), lambda i: (i, 0)),
            _const_spec(pool_w.shape),
            _const_spec(pool_scale.shape),
            _const_spec(w_out.shape),
        ],
        out_specs=pl.BlockSpec((tt, d), lambda i: (i, 0)),
        out_shape=jax.ShapeDtypeStruct((t, d), F32),
        scratch_shapes=[pltpu.VMEM((tt + HALO, pw), F32)],
        compiler_params=_cparams(("parallel",)),
        name="mix_out",
    )(pool, pool, fox, x2, pool_w, pool_scale, w_out)


def _mem_kv_kernel(m_ref, g_ref, wk_ref, wv_ref, k_ref, v_ref):
    m = _rms(m_ref[...], g_ref[...]).astype(BF16)
    k_ref[...] = jnp.dot(m, wk_ref[...].astype(BF16), preferred_element_type=F32).astype(BF16)
    v_ref[...] = jnp.dot(m, wv_ref[...].astype(BF16), preferred_element_type=F32).astype(BF16)


def _mem_kv(mem2, g, wk, wv, cb):
    n, d = mem2.shape
    return pl.pallas_call(
        _mem_kv_kernel,
        grid=(d // cb,),
        in_specs=[
            pl.BlockSpec((n, d), lambda j: (0, 0)),
            pl.BlockSpec((1, d), lambda j: (0, 0)),
            pl.BlockSpec((d, cb), lambda j: (0, j)),
            pl.BlockSpec((d, cb), lambda j: (0, j)),
        ],
        out_specs=[pl.BlockSpec((n, cb), lambda j: (0, j))] * 2,
        out_shape=[jax.ShapeDtypeStruct((n, d), BF16)] * 2,
        compiler_params=_cparams(("parallel",)),
        name="mem_kv",
    )(mem2, g, wk, wv)


def _xattn_kernel(h_ref, g_ref, wq_ref, k_ref, v_ref, wo_ref, o_ref, *, scale):
    h1 = h_ref[...]
    hn = _rms(h1, g_ref[...]).astype(BF16)
    q = jnp.dot(hn, wq_ref[...], preferred_element_type=F32).astype(BF16)
    d = q.shape[1]
    dh = d // XATTN_HEADS
    outs = []
    for hd in range(XATTN_HEADS):
        sl = slice(hd * dh, (hd + 1) * dh)
        s = lax.dot_general(q[:, sl], k_ref[:, sl], (((1,), (1,)), ((), ())),
                            preferred_element_type=F32) * scale
        p = jnp.exp(s - jnp.max(s, axis=1, keepdims=True))
        p = p / jnp.sum(p, axis=1, keepdims=True)
        outs.append(jnp.dot(p.astype(BF16), v_ref[:, sl], preferred_element_type=F32).astype(BF16))
    o = jnp.concatenate(outs, axis=1)
    o_ref[...] = h1 + jnp.dot(o, wo_ref[...], preferred_element_type=F32)


def _xattn(h1, g, wq, kmem, vmem, wo, s, n_mem, tt):
    t, d = h1.shape
    tps = s // tt
    kern = functools.partial(_xattn_kernel, scale=(d // XATTN_HEADS) ** -0.5)
    return pl.pallas_call(
        kern,
        grid=(t // tt,),
        in_specs=[
            pl.BlockSpec((tt, d), lambda i: (i, 0)),
            _const_spec(g.shape),
            _const_spec(wq.shape),
            pl.BlockSpec((n_mem, d), lambda i: (i // tps, 0)),
            pl.BlockSpec((n_mem, d), lambda i: (i // tps, 0)),
            _const_spec(wo.shape),
        ],
        out_specs=pl.BlockSpec((tt, d), lambda i: (i, 0)),
        out_shape=jax.ShapeDtypeStruct((t, d), F32),
        compiler_params=_cparams(("parallel",)),
        name="xattn",
    )(h1, g, wq, kmem, vmem, wo)


def _top16_rows(s, row_s, row_o):
    vals = jnp.zeros(row_o.shape, F32)
    idxs = jnp.zeros(row_o.shape, F32)
    for r in range(PEER_TOPK):
        m = jnp.max(s, axis=0, keepdims=True)
        am = jnp.min(jnp.where(s == m, row_s, 1e9), axis=0, keepdims=True)
        vals = jnp.where(row_o == r, m, vals)
        idxs = jnp.where(row_o == r, am, idxs)
        s = jnp.where(row_s == am, -jnp.inf, s)
    return vals, idxs


def _route_kernel(h_ref, g_ref, wq_ref, keys_ref, xn_ref, i_ref, j_ref, gate_ref, cand_sc):
    tt = h_ref.shape[0]
    xn = _rms(h_ref[...], g_ref[...]).astype(BF16)
    xn_ref[...] = xn
    q = jnp.dot(xn, wq_ref[...], preferred_element_type=F32).astype(BF16)
    row_k = lax.broadcasted_iota(jnp.int32, (PEER_KEYS, tt), 0).astype(F32)
    row_c = lax.broadcasted_iota(jnp.int32, (CAND_ROWS, tt), 0).astype(F32)
    row_o = lax.broadcasted_iota(jnp.int32, (PEER_TOPK, tt), 0)
    cand_sc[pl.ds(CAND_ROWS - SUBLANES, SUBLANES), :] = jnp.full((SUBLANES, tt), -jnp.inf, F32)
    ei_parts, ej_parts, gate_parts = [], [], []
    for hd in range(PEER_HEADS):
        tops = []
        for c in range(2):
            qc = q[:, (2 * hd + c) * LANES:(2 * hd + c + 1) * LANES]
            sc = lax.dot_general(keys_ref[2 * hd + c], qc, (((1,), (1,)), ((), ())),
                                 preferred_element_type=F32)
            tops.append(_top16_rows(sc, row_k, row_o))
        (s0, i0), (s1, i1) = tops
        for a in range(PEER_TOPK):
            n = CAND_COUNTS[a]
            cand_sc[pl.ds(CAND_OFFSETS[a], n), :] = s0[a:a + 1, :] + s1[0:n, :]
        best, prow = _top16_rows(cand_sc[...], row_c, row_o)
        a_sel = jnp.zeros_like(prow)
        off = jnp.zeros_like(prow)
        for a in range(1, PEER_TOPK):
            past = prow >= CAND_OFFSETS[a]
            a_sel = a_sel + jnp.where(past, 1.0, 0.0)
            off = off + jnp.where(past, float(CAND_COUNTS[a - 1]), 0.0)
        b_sel = prow - off
        ei = jnp.zeros_like(prow)
        ej = jnp.zeros_like(prow)
        for r in range(PEER_TOPK):
            ei = jnp.where(a_sel == r, i0[r:r + 1, :], ei)
            ej = jnp.where(b_sel == r, i1[r:r + 1, :], ej)
        e = jnp.exp(best - jnp.max(best, axis=0, keepdims=True))
        ei_parts.append(ei)
        ej_parts.append(ej)
        gate_parts.append(e / jnp.sum(e, axis=0, keepdims=True))
    i_ref[...] = jnp.concatenate(ei_parts, axis=0).T.astype(jnp.int32)
    j_ref[...] = jnp.concatenate(ej_parts, axis=0).T.astype(jnp.int32)
    gate_ref[...] = jnp.concatenate(gate_parts, axis=0).T


def _route(h2, g, wq, keys, tt):
    t, d = h2.shape
    return pl.pallas_call(
        _route_kernel,
        grid=(t // tt,),
        in_specs=[
            pl.BlockSpec((tt, d), lambda i: (i, 0)),
            _const_spec(g.shape),
            _const_spec(wq.shape),
            _const_spec(keys.shape),
        ],
        out_specs=[
            pl.BlockSpec((tt, d), lambda i: (i, 0)),
            pl.BlockSpec((tt, LANES), lambda i: (i, 0)),
            pl.BlockSpec((tt, LANES), lambda i: (i, 0)),
            pl.BlockSpec((tt, LANES), lambda i: (i, 0)),
        ],
        out_shape=[
            jax.ShapeDtypeStruct((t, d), BF16),
            jax.ShapeDtypeStruct((t, LANES), jnp.int32),
            jax.ShapeDtypeStruct((t, LANES), jnp.int32),
            jax.ShapeDtypeStruct((t, LANES), F32),
        ],
        scratch_shapes=[pltpu.VMEM((CAND_ROWS, tt), F32)],
        compiler_params=_cparams(("parallel",)),
        name="route",
    )(h2, g, wq, keys)


def _peer_act_kernel(xn_ref, u_ref, i_ref, j_ref, a_ref, *, eb):
    jb = pl.program_id(1)

    @pl.when(jb == 0)
    def _():
        a_ref[...] = jnp.zeros_like(a_ref)

    dense = lax.dot_general(xn_ref[...], u_ref[...], (((1,), (1,)), ((), ())),
                            preferred_element_type=F32)
    iv = i_ref[...]
    jv = j_ref[...]
    acc = a_ref[...]
    nsub = eb // PEER_KEYS
    for c in range(nsub):
        picked = jnp.take_along_axis(dense[:, c * PEER_KEYS:(c + 1) * PEER_KEYS], jv, axis=1)
        acc = jnp.where(iv == jb * nsub + c, picked, acc)
    a_ref[...] = acc


def _peer_act(xn, u, ei, ej, tt, eb):
    t, d = xn.shape
    ne = u.shape[0]
    return pl.pallas_call(
        functools.partial(_peer_act_kernel, eb=eb),
        grid=(t // tt, ne // eb),
        in_specs=[
            pl.BlockSpec((tt, d), lambda i, j: (i, 0)),
            pl.BlockSpec((eb, d), lambda i, j: (j, 0)),
            pl.BlockSpec((tt, LANES), lambda i, j: (i, 0)),
            pl.BlockSpec((tt, LANES), lambda i, j: (i, 0)),
        ],
        out_specs=pl.BlockSpec((tt, LANES), lambda i, j: (i, 0)),
        out_shape=jax.ShapeDtypeStruct((t, LANES), F32),
        compiler_params=_cparams(("parallel", "arbitrary")),
        name="peer_act",
    )(xn, u, ei, ej)


BUILD_UNROLL = 8


def _peer_out_kernel(h_ref, gate_ref, a_ref, i_ref, j_ref, v_ref, fn_ref, o_ref,
                     w_sc, wbuf_sc, acc_sc, *, tt, eb, stride):
    jb = pl.program_id(1)

    @pl.when(jb == 0)
    def _():
        a = a_ref[...]
        gelu = 0.5 * a * (1.0 + lax.erf(a * (2.0 ** -0.5)))
        w_sc[...] = gate_ref[...] * gelu
        acc_sc[...] = jnp.zeros_like(acc_sc)
        sub = lax.broadcasted_iota(jnp.int32, (PEER_KEYS, LANES), 0)

        def build(t, carry):
            irow = jnp.broadcast_to(i_ref[pl.ds(t, 1), :], (PEER_KEYS, LANES))
            jrow = jnp.broadcast_to(j_ref[pl.ds(t, 1), :], (PEER_KEYS, LANES))
            wrow = jnp.broadcast_to(w_sc[pl.ds(t, 1), :], (PEER_KEYS, LANES))
            first = jnp.where(sub == irow, wrow, 0.0).astype(BF16)
            second = jnp.where(sub == jrow, 1.0, 0.0).astype(BF16)
            wt = lax.dot_general(first, second, (((1,), (1,)), ((), ())),
                                 preferred_element_type=F32)
            wbuf_sc[pl.ds(t, PEER_KEYS, stride=stride), :] = wt
            return carry

        lax.fori_loop(0, tt, build, 0, unroll=BUILD_UNROLL)

    nsub = eb // PEER_KEYS
    parts = []
    for c in range(nsub):
        start = pl.multiple_of((jb * nsub + c) * stride, SUBLANES)
        parts.append(wbuf_sc[pl.ds(start, tt), :].astype(BF16))
    lhs = jnp.concatenate(parts, axis=1)
    acc_sc[...] += jnp.dot(lhs, v_ref[...], preferred_element_type=F32)

    @pl.when(jb == pl.num_programs(1) - 1)
    def _():
        o_ref[...] = _rms(h_ref[...] + acc_sc[...], fn_ref[...])


def _peer_out(h2, gate, a, ei, ej, v, fnorm, tt, eb):
    t, d = h2.shape
    ne = v.shape[0]
    stride = tt + SUBLANES
    kern = functools.partial(_peer_out_kernel, tt=tt, eb=eb, stride=stride)
    tok = lambda i, j: (i, 0)
    return pl.pallas_call(
        kern,
        grid=(t // tt, ne // eb),
        in_specs=[
            pl.BlockSpec((tt, d), tok),
            pl.BlockSpec((tt, LANES), tok),
            pl.BlockSpec((tt, LANES), tok),
            pl.BlockSpec((tt, LANES), tok),
            pl.BlockSpec((tt, LANES), tok),
            pl.BlockSpec((eb, d), lambda i, j: (j, 0)),
            pl.BlockSpec((1, d), lambda i, j: (0, 0)),
        ],
        out_specs=pl.BlockSpec((tt, d), tok),
        out_shape=jax.ShapeDtypeStruct((t, d), F32),
        scratch_shapes=[
            pltpu.VMEM((tt, LANES), F32),
            pltpu.VMEM((PEER_KEYS * stride, LANES), F32),
            pltpu.VMEM((tt, d), F32),
        ],
        compiler_params=_cparams(("parallel", "arbitrary")),
        name="peer_out",
    )(h2, gate, a, ei, ej, v, fnorm)


def kernel(x, mem, mix_norm, w_in, b_forget, pool_w, pool_scale, w_out, xattn_norm, mem_norm,
           xattn_wq, xattn_wk, xattn_wv, xattn_wo, peer_norm, peer_wq, peer_sub_keys, peer_u, peer_v,
           final_norm):
    b, s, d = x.shape
    n_mem = mem.shape[1]
    assert mix_norm.shape[0] == 1, "only depth 1 is supported"
    pw = pool_scale.shape[1]
    fox_w = (w_in.shape[2] - pw - FOX_HEADS) // 3
    assert fox_w == pw, "query columns must form one column block of the input projection"
    dh = fox_w // FOX_HEADS
    h = x.reshape(b * s, d)
    mem2 = mem.reshape(b * n_mem, d)

    w_main = w_in[0][:, :pw + 3 * fox_w].astype(BF16)
    wf = jnp.pad(w_in[0][:, pw + 3 * fox_w:], ((0, 0), (0, LANES - FOX_HEADS))).astype(BF16)
    bf = jnp.pad(b_forget[0], (0, LANES - FOX_HEADS)).reshape(1, LANES)
    pool, qkv, f = _in_proj(h, mix_norm.reshape(1, d), w_main, wf, bf, tt=512, cb=pw,
                            q_scale=dh ** -0.5 * LOG2E)
    qx, kx = _decay(f, b, s, dh, rb=256)
    y_fox = _fox(qkv, qx, kx, b, s, dh, tq=512, hg=4)
    h1 = _mix_out(pool, y_fox, h, pool_w[0].astype(BF16), pool_scale.reshape(1, pw),
                  w_out[0].astype(BF16), s, tt=256)

    kmem, vmem = _mem_kv(mem2, mem_norm.reshape(1, d), xattn_wk[0], xattn_wv[0], cb=512)
    h2 = _xattn(h1, xattn_norm.reshape(1, d), xattn_wq[0].astype(BF16), kmem, vmem,
                xattn_wo[0].astype(BF16), s, n_mem, tt=256)

    keys = peer_sub_keys[0].reshape(2 * PEER_HEADS, PEER_KEYS, -1).astype(BF16)
    xn, ei, ej, gate = _route(h2, peer_norm.reshape(1, d), peer_wq[0].astype(BF16), keys, tt=256)
    a = _peer_act(xn, peer_u[0].astype(BF16), ei, ej, tt=1024, eb=512)
    out = _peer_out(h2, gate, a, ei, ej, peer_v[0].astype(BF16), final_norm.reshape(1, d), tt=256, eb=1024)
    return out.reshape(b, s, d)
```

```python
import functools
import math

import numpy as np
import jax
import jax.numpy as jnp
from jax import lax
from jax.experimental import pallas as pl
from jax.experimental.pallas import tpu as pltpu

EPS = 1e-6
POOL_WINDOWS = (2, 4, 8, 16)
FOX_HEADS = 8
XATTN_HEADS = 4
PEER_HEADS = 8
PEER_KEYS = 128
PEER_TOPK = 16
LANES = 128
SUBLANES = 8
HALO = 16
NEG = -1e30
LOG2E = math.log2(math.e)
VMEM_LIMIT = 56 * 1024 * 1024

F32 = jnp.float32
BF16 = jnp.bfloat16

CAND_COUNTS = tuple(PEER_TOPK // (a + 1) for a in range(PEER_TOPK))
CAND_OFFSETS = tuple(sum(CAND_COUNTS[:a]) for a in range(PEER_TOPK))
CAND_TOTAL = sum(CAND_COUNTS)
CAND_ROWS = -(-CAND_TOTAL // SUBLANES) * SUBLANES


def _cparams(sem):
    return pltpu.CompilerParams(dimension_semantics=sem, vmem_limit_bytes=VMEM_LIMIT)


def _rms(xf, g):
    return xf * lax.rsqrt(jnp.mean(xf * xf, axis=-1, keepdims=True) + EPS) * g


def _const_spec(shape):
    nd = len(shape)
    return pl.BlockSpec(shape, lambda *_: (0,) * nd, pipeline_mode=pl.Buffered(1))


def _in_proj_kernel(x_ref, g_ref, w_ref, wf_ref, bf_ref, pool_ref, qkv_ref, f_ref, a_sc, *, q_scale):
    j = pl.program_id(1)

    @pl.when(j == 0)
    def _():
        a = _rms(x_ref[...], g_ref[...]).astype(BF16)
        a_sc[...] = a
        f_ref[...] = jnp.dot(a, wf_ref[...], preferred_element_type=F32) + bf_ref[...]
        pool_ref[...] = jnp.dot(a, w_ref[...], preferred_element_type=F32)

    @pl.when(j > 0)
    def _():
        res = jnp.dot(a_sc[...], w_ref[...], preferred_element_type=F32)
        qkv_ref[...] = (res * jnp.where(j == 1, q_scale, 1.0)).astype(BF16)


def _in_proj(x2, g, w_main, wf, bf, tt, cb, q_scale):
    t, d = x2.shape
    ncb = w_main.shape[1] // cb
    return pl.pallas_call(
        functools.partial(_in_proj_kernel, q_scale=q_scale),
        grid=(t // tt, ncb),
        in_specs=[
            pl.BlockSpec((tt, d), lambda i, j: (i, 0)),
            pl.BlockSpec((1, d), lambda i, j: (0, 0)),
            pl.BlockSpec((d, cb), lambda i, j: (0, j)),
            pl.BlockSpec((d, LANES), lambda i, j: (0, 0)),
            pl.BlockSpec((1, LANES), lambda i, j: (0, 0)),
        ],
        out_specs=[
            pl.BlockSpec((tt, cb), lambda i, j: (i, 0)),
            pl.BlockSpec((tt, cb), lambda i, j: (i, jnp.maximum(j - 1, 0))),
            pl.BlockSpec((tt, LANES), lambda i, j: (i, 0)),
        ],
        out_shape=[
            jax.ShapeDtypeStruct((t, cb), F32),
            jax.ShapeDtypeStruct((t, (ncb - 1) * cb), BF16),
            jax.ShapeDtypeStruct((t, LANES), F32),
        ],
        scratch_shapes=[pltpu.VMEM((tt, d), BF16)],
        compiler_params=_cparams(("parallel", "arbitrary")),
        name="in_proj",
    )(x2, g, w_main, wf, bf)


C_PARTS = 3


def _decay_kernel(f_ref, eq_ref, ek_ref, oq_ref, ok_ref, qx_ref, kx_ref, carry_sc, *, rb):
    r = pl.program_id(1)

    @pl.when(r == 0)
    def _():
        carry_sc[...] = jnp.zeros_like(carry_sc)

    f = f_ref[...]
    ls = jnp.minimum(f, 0.0) - jnp.log1p(jnp.exp(-jnp.abs(f)))
    row = lax.broadcasted_iota(jnp.int32, (rb, rb), 0)
    col = lax.broadcasted_iota(jnp.int32, (rb, rb), 1)
    tri = (col <= row).astype(F32)
    c = jnp.dot(tri, ls, preferred_element_type=F32, precision=lax.Precision.HIGHEST) + carry_sc[...]
    carry_sc[...] = c[rb - 1:rb, :]
    rest = c * LOG2E
    parts = []
    for _ in range(C_PARTS):
        piece = rest.astype(BF16)
        parts.append(piece)
        rest = rest - piece.astype(F32)
    parts = jnp.concatenate(parts, axis=1)
    qx_ref[...] = (jnp.dot(parts, eq_ref[...], preferred_element_type=F32) + oq_ref[...]).astype(BF16)
    kx_ref[...] = (ok_ref[...] - jnp.dot(parts, ek_ref[...], preferred_element_type=F32)).astype(BF16)


def _decay_constants(dh):
    eq = np.zeros((C_PARTS * LANES, FOX_HEADS * dh), np.float32)
    ek = np.zeros_like(eq)
    oq = np.zeros((1, FOX_HEADS * dh), np.float32)
    ok = np.zeros_like(oq)
    for h in range(FOX_HEADS):
        for p in range(C_PARTS):
            eq[p * LANES + h, h * dh + p] = 1.0
            ek[p * LANES + h, h * dh + C_PARTS + p] = 1.0
            oq[0, h * dh + C_PARTS + p] = 1.0
            ok[0, h * dh + p] = 1.0
    return jnp.asarray(eq, BF16), jnp.asarray(ek, BF16), jnp.asarray(oq), jnp.asarray(ok)


def _decay(f, b, s, dh, rb):
    t = f.shape[0]
    nr = s // rb
    consts = _decay_constants(dh)
    w = FOX_HEADS * dh
    return pl.pallas_call(
        functools.partial(_decay_kernel, rb=rb),
        grid=(b, nr),
        in_specs=[pl.BlockSpec((rb, LANES), lambda bi, r: (bi * nr + r, 0))]
        + [_const_spec(c.shape) for c in consts],
        out_specs=[pl.BlockSpec((rb, w), lambda bi, r: (bi * nr + r, 0))] * 2,
        out_shape=[jax.ShapeDtypeStruct((t, w), BF16)] * 2,
        scratch_shapes=[pltpu.VMEM((1, LANES), F32)],
        compiler_params=_cparams(("parallel", "arbitrary")),
        name="decay",
    )(f, *consts)


def _fox_kernel(q_ref, qx_ref, k_ref, kx_ref, v_ref, o_ref, m_sc, acc_sc, *, tq, dh, hg):
    qi = pl.program_id(2)
    ki = pl.program_id(3)

    @pl.when(ki == 0)
    def _():
        m_sc[...] = jnp.full_like(m_sc, NEG)
        acc_sc[...] = jnp.zeros_like(acc_sc)

    def step(diagonal):
        if diagonal:
            row = lax.broadcasted_iota(jnp.int32, (tq, tq), 0)
            col = lax.broadcasted_iota(jnp.int32, (tq, tq), 1)
            keep = col <= row
        ones = jnp.ones((tq, dh), BF16)
        for hh in range(hg):
            sl = slice(hh * dh, (hh + 1) * dh)
            qa = jnp.concatenate([q_ref[:, sl], qx_ref[:, sl]], axis=1)
            ka = jnp.concatenate([k_ref[:, sl], kx_ref[:, sl]], axis=1)
            s = lax.dot_general(qa, ka, (((1,), (1,)), ((), ())), preferred_element_type=F32)
            if diagonal:
                s = jnp.where(keep, s, NEG)
            m_prev = m_sc[hh]
            m_cur = jnp.broadcast_to(jnp.max(s, axis=1, keepdims=True), m_prev.shape)
            m_new = jnp.maximum(m_prev, m_cur)
            alpha = jnp.exp2(m_prev - m_new)
            p = jnp.exp2(s - jnp.concatenate([m_new] * (tq // LANES), axis=1))
            va = jnp.concatenate([v_ref[:, sl], ones], axis=1)
            acc_sc[hh] = (jnp.concatenate([alpha] * (2 * dh // LANES), axis=1) * acc_sc[hh]
                          + jnp.dot(p.astype(BF16), va, preferred_element_type=F32))
            m_sc[hh] = m_new

    @pl.when(ki < qi)
    def _():
        step(False)

    @pl.when(ki == qi)
    def _():
        step(True)

    @pl.when(ki == pl.num_programs(3) - 1)
    def _():
        for hh in range(hg):
            acc = acc_sc[hh]
            o_ref[:, hh * dh:(hh + 1) * dh] = (acc[:, :dh] / acc[:, dh:]).astype(o_ref.dtype)


def _fox(qkv, qx, kx, b, s, dh, tq, hg):
    t = qkv.shape[0]
    ng = FOX_HEADS // hg
    nq = s // tq
    kern = functools.partial(_fox_kernel, tq=tq, dh=dh, hg=hg)
    qrow = lambda bi, g, qi, ki: (bi * nq + qi, g)
    krow = lambda off: (lambda bi, g, qi, ki: (bi * nq + jnp.minimum(ki, qi), off + g))
    blk = (tq, hg * dh)
    return pl.pallas_call(
        kern,
        grid=(b, ng, nq, nq),
        in_specs=[
            pl.BlockSpec(blk, qrow),
            pl.BlockSpec(blk, qrow),
            pl.BlockSpec(blk, krow(ng)),
            pl.BlockSpec(blk, krow(0)),
            pl.BlockSpec(blk, krow(2 * ng)),
        ],
        out_specs=pl.BlockSpec(blk, qrow),
        out_shape=jax.ShapeDtypeStruct((t, FOX_HEADS * dh), BF16),
        scratch_shapes=[
            pltpu.VMEM((hg, tq, LANES), F32),
            pltpu.VMEM((hg, tq, 2 * dh), F32),
        ],
        compiler_params=_cparams(("parallel", "parallel", "parallel", "arbitrary")),
        name="fox",
    )(qkv, qx, qkv, kx, qkv)


def _mix_out_kernel(pool_ref, halo_ref, fox_ref, x_ref, pw_ref, ps_ref, wo_ref, o_ref, ext_sc, *, tt, tiles_per_seq):
    i = pl.program_id(0)
    seq_tile = i % tiles_per_seq
    cur = pool_ref[...]
    ext_sc[pl.ds(HALO, tt), :] = cur
    ext_sc[pl.ds(0, HALO), :] = jnp.where(seq_tile == 0, 0.0, halo_ref[...])
    pos = seq_tile * tt + lax.broadcasted_iota(jnp.int32, (tt, 1), 0)
    pw = pool_ref.shape[1]
    grp = pw // len(POOL_WINDOWS)
    ys = []
    for g, w in enumerate(POOL_WINDOWS):
        cols = pl.ds(g * grp, grp)
        wsum = cur[:, g * grp:(g + 1) * grp]
        for k in range(1, w):
            wsum = wsum + ext_sc[pl.ds(HALO - k, tt), cols]
        count = jnp.minimum(pos + 1, w).astype(F32)
        diff = wsum / count - cur[:, g * grp:(g + 1) * grp]
        ys.append(jnp.dot(diff.astype(BF16), pw_ref[g], preferred_element_type=F32))
    y_pool = jnp.concatenate(ys, axis=1) * ps_ref[...]
    mixed = jnp.concatenate([y_pool.astype(BF16), fox_ref[...]], axis=1)
    o_ref[...] = x_ref[...] + jnp.dot(mixed, wo_ref[...], preferred_element_type=F32)


def _mix_out(pool, fox, x2, pool_w, pool_scale, w_out, s, tt):
    t, pw = pool.shape
    d = x2.shape[1]
    hb = tt // HALO
    kern = functools.partial(_mix_out_kernel, tt=tt, tiles_per_seq=s // tt)
    return pl.pallas_call(
        kern,
        grid=(t // tt,),
        in_specs=[
            pl.BlockSpec((tt, pw), lambda i: (i, 0)),
            pl.BlockSpec((HALO, pw), lambda i: (jnp.maximum(i * hb - 1, 0), 0)),
            pl.BlockSpec((tt, fox.shape[1]), lambda i: (i, 0)),
            pl.BlockSpec((tt, d), lambda i: (i, 0)),
            _const_spec(pool_w.shape),
            _const_spec(pool_scale.shape),
            _const_spec(w_out.shape),
        ],
        out_specs=pl.BlockSpec((tt, d), lambda i: (i, 0)),
        out_shape=jax.ShapeDtypeStruct((t, d), F32),
        scratch_shapes=[pltpu.VMEM((tt + HALO, pw), F32)],
        compiler_params=_cparams(("parallel",)),
        name="mix_out",
    )(pool, pool, fox, x2, pool_w, pool_scale, w_out)


def _mem_kv_kernel(m_ref, g_ref, wk_ref, wv_ref, k_ref, v_ref):
    m = _rms(m_ref[...], g_ref[...]).astype(BF16)
    k_ref[...] = jnp.dot(m, wk_ref[...].astype(BF16), preferred_element_type=F32).astype(BF16)
    v_ref[...] = jnp.dot(m, wv_ref[...].astype(BF16), preferred_element_type=F32).astype(BF16)


def _mem_kv(mem2, g, wk, wv, cb):
    n, d = mem2.shape
    return pl.pallas_call(
        _mem_kv_kernel,
        grid=(d // cb,),
        in_specs=[
            pl.BlockSpec((n, d), lambda j: (0, 0)),
            pl.BlockSpec((1, d), lambda j: (0, 0)),
            pl.BlockSpec((d, cb), lambda j: (0, j)),
            pl.BlockSpec((d, cb), lambda j: (0, j)),
        ],
        out_specs=[pl.BlockSpec((n, cb), lambda j: (0, j))] * 2,
        out_shape=[jax.ShapeDtypeStruct((n, d), BF16)] * 2,
        compiler_params=_cparams(("parallel",)),
        name="mem_kv",
    )(mem2, g, wk, wv)


def _xattn_kernel(h_ref, g_ref, wq_ref, k_ref, v_ref, wo_ref, o_ref, *, scale):
    h1 = h_ref[...]
    hn = _rms(h1, g_ref[...]).astype(BF16)
    q = jnp.dot(hn, wq_ref[...], preferred_element_type=F32).astype(BF16)
    d = q.shape[1]
    dh = d // XATTN_HEADS
    outs = []
    for hd in range(XATTN_HEADS):
        sl = slice(hd * dh, (hd + 1) * dh)
        s = lax.dot_general(q[:, sl], k_ref[:, sl], (((1,), (1,)), ((), ())),
                            preferred_element_type=F32) * scale
        p = jnp.exp(s - jnp.max(s, axis=1, keepdims=True))
        p = p / jnp.sum(p, axis=1, keepdims=True)
        outs.append(jnp.dot(p.astype(BF16), v_ref[:, sl], preferred_element_type=F32).astype(BF16))
    o = jnp.concatenate(outs, axis=1)
    o_ref[...] = h1 + jnp.dot(o, wo_ref[...], preferred_element_type=F32)


def _xattn(h1, g, wq, kmem, vmem, wo, s, n_mem, tt):
    t, d = h1.shape
    tps = s // tt
    kern = functools.partial(_xattn_kernel, scale=(d // XATTN_HEADS) ** -0.5)
    return pl.pallas_call(
        kern,
        grid=(t // tt,),
        in_specs=[
            pl.BlockSpec((tt, d), lambda i: (i, 0)),
            _const_spec(g.shape),
            _const_spec(wq.shape),
            pl.BlockSpec((n_mem, d), lambda i: (i // tps, 0)),
            pl.BlockSpec((n_mem, d), lambda i: (i // tps, 0)),
            _const_spec(wo.shape),
        ],
        out_specs=pl.BlockSpec((tt, d), lambda i: (i, 0)),
        out_shape=jax.ShapeDtypeStruct((t, d), F32),
        compiler_params=_cparams(("parallel",)),
        name="xattn",
    )(h1, g, wq, kmem, vmem, wo)


def _top16_rows(s, row_s, row_o):
    vals = jnp.zeros(row_o.shape, F32)
    idxs = jnp.zeros(row_o.shape, F32)
    for r in range(PEER_TOPK):
        m = jnp.max(s, axis=0, keepdims=True)
        am = jnp.min(jnp.where(s == m, row_s, 1e9), axis=0, keepdims=True)
        vals = jnp.where(row_o == r, m, vals)
        idxs = jnp.where(row_o == r, am, idxs)
        s = jnp.where(row_s == am, -jnp.inf, s)
    return vals, idxs


def _route_kernel(h_ref, g_ref, wq_ref, keys_ref, xn_ref, i_ref, j_ref, gate_ref, cand_sc):
    tt = h_ref.shape[0]
    xn = _rms(h_ref[...], g_ref[...]).astype(BF16)
    xn_ref[...] = xn
    q = jnp.dot(xn, wq_ref[...], preferred_element_type=F32).astype(BF16)
    row_k = lax.broadcasted_iota(jnp.int32, (PEER_KEYS, tt), 0).astype(F32)
    row_c = lax.broadcasted_iota(jnp.int32, (CAND_ROWS, tt), 0).astype(F32)
    row_o = lax.broadcasted_iota(jnp.int32, (PEER_TOPK, tt), 0)
    cand_sc[pl.ds(CAND_ROWS - SUBLANES, SUBLANES), :] = jnp.full((SUBLANES, tt), -jnp.inf, F32)
    ei_parts, ej_parts, gate_parts = [], [], []
    for hd in range(PEER_HEADS):
        tops = []
        for c in range(2):
            qc = q[:, (2 * hd + c) * LANES:(2 * hd + c + 1) * LANES]
            sc = lax.dot_general(keys_ref[2 * hd + c], qc, (((1,), (1,)), ((), ())),
                                 preferred_element_type=F32)
            tops.append(_top16_rows(sc, row_k, row_o))
        (s0, i0), (s1, i1) = tops
        for a in range(PEER_TOPK):
            n = CAND_COUNTS[a]
            cand_sc[pl.ds(CAND_OFFSETS[a], n), :] = s0[a:a + 1, :] + s1[0:n, :]
        best, prow = _top16_rows(cand_sc[...], row_c, row_o)
        a_sel = jnp.zeros_like(prow)
        off = jnp.zeros_like(prow)
        for a in range(1, PEER_TOPK):
            past = prow >= CAND_OFFSETS[a]
            a_sel = a_sel + jnp.where(past, 1.0, 0.0)
            off = off + jnp.where(past, float(CAND_COUNTS[a - 1]), 0.0)
        b_sel = prow - off
        ei = jnp.zeros_like(prow)
        ej = jnp.zeros_like(prow)
        for r in range(PEER_TOPK):
            ei = jnp.where(a_sel == r, i0[r:r + 1, :], ei)
            ej = jnp.where(b_sel == r, i1[r:r + 1, :], ej)
        e = jnp.exp(best - jnp.max(best, axis=0, keepdims=True))
        ei_parts.append(ei)
        ej_parts.append(ej)
        gate_parts.append(e / jnp.sum(e, axis=0, keepdims=True))
    i_ref[...] = jnp.concatenate(ei_parts, axis=0).T.astype(jnp.int32)
    j_ref[...] = jnp.concatenate(ej_parts, axis=0).T.astype(jnp.int32)
    gate_ref[...] = jnp.concatenate(gate_parts, axis=0).T


def _route(h2, g, wq, keys, tt):
    t, d = h2.shape
    return pl.pallas_call(
        _route_kernel,
        grid=(t // tt,),
        in_specs=[
            pl.BlockSpec((tt, d), lambda i: (i, 0)),
            _const_spec(g.shape),
            _const_spec(wq.shape),
            _const_spec(keys.shape),
        ],
        out_specs=[
            pl.BlockSpec((tt, d), lambda i: (i, 0)),
            pl.BlockSpec((tt, LANES), lambda i: (i, 0)),
            pl.BlockSpec((tt, LANES), lambda i: (i, 0)),
            pl.BlockSpec((tt, LANES), lambda i: (i, 0)),
        ],
        out_shape=[
            jax.ShapeDtypeStruct((t, d), BF16),
            jax.ShapeDtypeStruct((t, LANES), jnp.int32),
            jax.ShapeDtypeStruct((t, LANES), jnp.int32),
            jax.ShapeDtypeStruct((t, LANES), F32),
        ],
        scratch_shapes=[pltpu.VMEM((CAND_ROWS, tt), F32)],
        compiler_params=_cparams(("parallel",)),
        name="route",
    )(h2, g, wq, keys)


def _peer_act_kernel(xn_ref, u_ref, i_ref, j_ref, a_ref, *, eb):
    jb = pl.program_id(1)

    @pl.when(jb == 0)
    def _():
        a_ref[...] = jnp.zeros_like(a_ref)

    dense = lax.dot_general(xn_ref[...], u_ref[...].astype(BF16), (((1,), (1,)), ((), ())),
                            preferred_element_type=F32)
    iv = i_ref[...]
    jv = j_ref[...]
    acc = a_ref[...]
    nsub = eb // PEER_KEYS
    for c in range(nsub):
        picked = jnp.take_along_axis(dense[:, c * PEER_KEYS:(c + 1) * PEER_KEYS], jv, axis=1)
        acc = jnp.where(iv == jb * nsub + c, picked, acc)
    a_ref[...] = acc


def _peer_act(xn, u, ei, ej, tt, eb):
    t, d = xn.shape
    ne = u.shape[0]
    return pl.pallas_call(
        functools.partial(_peer_act_kernel, eb=eb),
        grid=(t // tt, ne // eb),
        in_specs=[
            pl.BlockSpec((tt, d), lambda i, j: (i, 0)),
            pl.BlockSpec((eb, d), lambda i, j: (j, 0)),
            pl.BlockSpec((tt, LANES), lambda i, j: (i, 0)),
            pl.BlockSpec((tt, LANES), lambda i, j: (i, 0)),
        ],
        out_specs=pl.BlockSpec((tt, LANES), lambda i, j: (i, 0)),
        out_shape=jax.ShapeDtypeStruct((t, LANES), F32),
        compiler_params=_cparams(("parallel", "arbitrary")),
        name="peer_act",
    )(xn, u, ei, ej)


BUILD_UNROLL = 16


def _peer_out_kernel(h_ref, gate_ref, a_ref, i_ref, j_ref, v_ref, fn_ref, o_ref,
                     w_sc, wbuf_sc, acc_sc, *, tt, eb, stride):
    jb = pl.program_id(1)

    @pl.when(jb == 0)
    def _():
        a = a_ref[...]
        gelu = 0.5 * a * (1.0 + lax.erf(a * (2.0 ** -0.5)))
        w_sc[...] = gate_ref[...] * gelu
        acc_sc[...] = jnp.zeros_like(acc_sc)
        sub = lax.broadcasted_iota(jnp.int32, (PEER_KEYS, LANES), 0)

        def build(t, carry):
            irow = jnp.broadcast_to(i_ref[pl.ds(t, 1), :], (PEER_KEYS, LANES))
            jrow = jnp.broadcast_to(j_ref[pl.ds(t, 1), :], (PEER_KEYS, LANES))
            wrow = jnp.broadcast_to(w_sc[pl.ds(t, 1), :], (PEER_KEYS, LANES))
            first = jnp.where(sub == irow, wrow, 0.0).astype(BF16)
            second = jnp.where(sub == jrow, 1.0, 0.0).astype(BF16)
            wt = lax.dot_general(first, second, (((1,), (1,)), ((), ())),
                                 preferred_element_type=F32)
            wbuf_sc[pl.ds(t, PEER_KEYS, stride=stride), :] = wt
            return carry

        lax.fori_loop(0, tt, build, 0, unroll=BUILD_UNROLL)

    nsub = eb // PEER_KEYS
    parts = []
    for c in range(nsub):
        start = pl.multiple_of((jb * nsub + c) * stride, SUBLANES)
        parts.append(wbuf_sc[pl.ds(start, tt), :].astype(BF16))
    lhs = jnp.concatenate(parts, axis=1)
    acc_sc[...] += jnp.dot(lhs, v_ref[...], preferred_element_type=F32)

    @pl.when(jb == pl.num_programs(1) - 1)
    def _():
        o_ref[...] = _rms(h_ref[...] + acc_sc[...], fn_ref[...])


def _peer_out(h2, gate, a, ei, ej, v, fnorm, tt, eb):
    t, d = h2.shape
    ne = v.shape[0]
    stride = tt + SUBLANES
    kern = functools.partial(_peer_out_kernel, tt=tt, eb=eb, stride=stride)
    tok = lambda i, j: (i, 0)
    return pl.pallas_call(
        kern,
        grid=(t // tt, ne // eb),
        in_specs=[
            pl.BlockSpec((tt, d), tok, pipeline_mode=pl.Buffered(1)),
            pl.BlockSpec((tt, LANES), tok),
            pl.BlockSpec((tt, LANES), tok),
            pl.BlockSpec((tt, LANES), tok),
            pl.BlockSpec((tt, LANES), tok),
            pl.BlockSpec((eb, d), lambda i, j: (j, 0)),
            pl.BlockSpec((1, d), lambda i, j: (0, 0)),
        ],
        out_specs=pl.BlockSpec((tt, d), tok, pipeline_mode=pl.Buffered(1)),
        out_shape=jax.ShapeDtypeStruct((t, d), F32),
        scratch_shapes=[
            pltpu.VMEM((tt, LANES), F32),
            pltpu.VMEM((PEER_KEYS * stride, LANES), F32),
            pltpu.VMEM((tt, d), F32),
        ],
        compiler_params=_cparams(("parallel", "arbitrary")),
        name="peer_out",
    )(h2, gate, a, ei, ej, v, fnorm)


def kernel(x, mem, mix_norm, w_in, b_forget, pool_w, pool_scale, w_out, xattn_norm, mem_norm,
           xattn_wq, xattn_wk, xattn_wv, xattn_wo, peer_norm, peer_wq, peer_sub_keys, peer_u, peer_v,
           final_norm):
    b, s, d = x.shape
    n_mem = mem.shape[1]
    assert mix_norm.shape[0] == 1, "only depth 1 is supported"
    pw = pool_scale.shape[1]
    fox_w = (w_in.shape[2] - pw - FOX_HEADS) // 3
    assert fox_w == pw, "query columns must form one column block of the input projection"
    dh = fox_w // FOX_HEADS
    h = x.reshape(b * s, d)
    mem2 = mem.reshape(b * n_mem, d)

    w_main = w_in[0][:, :pw + 3 * fox_w].astype(BF16)
    wf = jnp.pad(w_in[0][:, pw + 3 * fox_w:], ((0, 0), (0, LANES - FOX_HEADS))).astype(BF16)
    bf = jnp.pad(b_forget[0], (0, LANES - FOX_HEADS)).reshape(1, LANES)
    pool, qkv, f = _in_proj(h, mix_norm.reshape(1, d), w_main, wf, bf, tt=512, cb=pw,
                            q_scale=dh ** -0.5 * LOG2E)
    qx, kx = _decay(f, b, s, dh, rb=256)
    y_fox = _fox(qkv, qx, kx, b, s, dh, tq=512, hg=4)
    h1 = _mix_out(pool, y_fox, h, pool_w[0].astype(BF16), pool_scale.reshape(1, pw),
                  w_out[0].astype(BF16), s, tt=256)

    kmem, vmem = _mem_kv(mem2, mem_norm.reshape(1, d), xattn_wk[0], xattn_wv[0], cb=512)
    h2 = _xattn(h1, xattn_norm.reshape(1, d), xattn_wq[0].astype(BF16), kmem, vmem,
                xattn_wo[0].astype(BF16), s, n_mem, tt=256)

    keys = peer_sub_keys[0].reshape(2 * PEER_HEADS, PEER_KEYS, -1).astype(BF16)
    xn, ei, ej, gate = _route(h2, peer_norm.reshape(1, d), peer_wq[0].astype(BF16), keys, tt=256)
    a = _peer_act(xn, peer_u[0], ei, ej, tt=2048, eb=512)
    out = _peer_out(h2, gate, a, ei, ej, peer_v[0].astype(BF16), final_norm.reshape(1, d), tt=512, eb=512)
    return out.reshape(b, s, d)
```
